```python
import jax, jax.numpy as jnp
from jax import lax
import numpy as np

D_MODEL = 1024
BATCH = 2
SEQ = 16384
DEPTH = 2
DEC_BATCH = 16
DEC_SEQ = 2048
PAST_LEN = 128

N_BRANCH = 4
W_BRANCH = 512
POOL_GROUPS = 4
POOL_WINDOWS = (2, 4, 8, 16)
POOL_GW = W_BRANCH // POOL_GROUPS
FNET_HEADS = 4
FNET_HW = W_BRANCH // FNET_HEADS
CONV_WIDTH = 31
SCONV_WIDTH = 3
IN_COLS = W_BRANCH * 7
N_KEYS = 128
N_EXPERTS = N_KEYS * N_KEYS
PEER_HEADS = 8
PEER_TOPK = 16
D_QUERY = 256
D_HALF = D_QUERY // 2
TOKEN_BLOCK = 128
N_MOD = 6
EPS = 1e-6

kernel_name = "hybrid_pool_fourier_conv_peer_encoder"


def rmsnorm(x, g):
    xf = x.astype(jnp.float32)
    r = lax.rsqrt(jnp.mean(xf * xf, axis=-1, keepdims=True) + EPS)
    return (xf * r * g.astype(jnp.float32)).astype(x.dtype)


def layernorm(x, g, b):
    xf = x.astype(jnp.float32)
    mu = jnp.mean(xf, axis=-1, keepdims=True)
    var = jnp.mean(jnp.square(xf - mu), axis=-1, keepdims=True)
    y = (xf - mu) * lax.rsqrt(var + EPS) * g.astype(jnp.float32) + b.astype(jnp.float32)
    return y.astype(x.dtype)


def depthwise_conv(x, w, pad):
    rhs = w[:, None, :].astype(x.dtype)
    return lax.conv_general_dilated(x, rhs, window_strides=(1,), padding=[(pad, pad)],
                                    dimension_numbers=("NWC", "WIO", "NWC"),
                                    feature_group_count=x.shape[-1])


def pool_mixer(z, maps, scale):
    S = z.shape[1]
    zf = z.astype(jnp.float32)
    cs0 = jnp.pad(jnp.cumsum(zf, axis=1), ((0, 0), (1, 0), (0, 0)))
    t = jnp.arange(S)
    outs = []
    for gi, w in enumerate(POOL_WINDOWS):
        half = w // 2
        lo = jnp.clip(t - half, 0, S)
        hi = jnp.clip(t + half, 0, S)
        sl = cs0[:, :, gi * POOL_GW:(gi + 1) * POOL_GW]
        cnt = (hi - lo).astype(jnp.float32)[None, :, None]
        mean = (jnp.take(sl, hi, axis=1) - jnp.take(sl, lo, axis=1)) / cnt
        outs.append(mean - zf[:, :, gi * POOL_GW:(gi + 1) * POOL_GW])
    p = jnp.stack(outs, axis=2).astype(z.dtype)
    y = jnp.einsum('bsgc,gcd->bsgd', p, maps)
    return y.reshape(z.shape) * scale


def fourier_mixer(z):
    B, S, _ = z.shape
    zh = z.astype(jnp.float32).reshape(B, S, FNET_HEADS, FNET_HW)
    f = jnp.fft.fft2(zh, axes=(1, 3), norm="ortho").real
    return f.reshape(B, S, W_BRANCH).astype(z.dtype)


def conformer_conv(z2, dw_w, dw_b, ln_g, ln_b):
    a, b = jnp.split(z2, 2, axis=-1)
    u = a * jax.nn.sigmoid(b)
    u = depthwise_conv(u, dw_w, CONV_WIDTH // 2) + dw_b
    u = layernorm(u, ln_g, ln_b)
    return jax.nn.silu(u)


def short_gated_conv(zb, zc, zx, w):
    return zb * depthwise_conv(zc * zx, w, SCONV_WIDTH // 2)


def peer(h, w_q, sub_keys, u_tab, v_tab):
    B, S, D = h.shape
    blocks = h.reshape(-1, TOKEN_BLOCK, D)

    def one_block(t):
        q = (t @ w_q).reshape(TOKEN_BLOCK, PEER_HEADS, 2, D_HALF)
        s = jnp.einsum('thpd,hpnd->thpn', q.astype(jnp.float32), sub_keys.astype(jnp.float32))
        vals, idx = lax.top_k(s, PEER_TOPK)
        cand = (vals[:, :, 0, :, None] + vals[:, :, 1, None, :]).reshape(
            TOKEN_BLOCK, PEER_HEADS, PEER_TOPK * PEER_TOPK)
        best, ci = lax.top_k(cand, PEER_TOPK)
        e = (jnp.take_along_axis(idx[:, :, 0], ci // PEER_TOPK, axis=-1) * N_KEYS
             + jnp.take_along_axis(idx[:, :, 1], ci % PEER_TOPK, axis=-1))
        g = jax.nn.softmax(best, axis=-1)
        ug = jnp.take(u_tab, e, axis=0)
        a = jnp.einsum('thkd,td->thk', ug, t)
        hid = (jax.nn.gelu(a.astype(jnp.float32)) * g).astype(t.dtype)
        vg = jnp.take(v_tab, e, axis=0)
        return jnp.einsum('thk,thkd->td', hid, vg)

    return lax.map(one_block, blocks).reshape(B, S, D)


def trunk(x, c, w_ada, b_ada, g_norm_mix, w_in, pool_maps, pool_scale, conv_dw_w, conv_dw_b,
          conv_ln_g, conv_ln_b, sconv_w, w_branch, w_gate, w_out, g_norm_ffn, w_q, sub_keys,
          expert_u, expert_v, g_final):
    Bn = x.shape[0]
    c_act = jax.nn.silu(c)
    for l in range(DEPTH):
        mod = (c_act @ w_ada[l] + b_ada[l]).reshape(Bn, N_MOD, D_MODEL)
        sh1, sc1, gt1, sh2, sc2, gt2 = [mod[:, i, None, :] for i in range(N_MOD)]
        h = rmsnorm(x, g_norm_mix[l]) * (1.0 + sc1) + sh1
        z = h @ w_in[l]
        W = W_BRANCH
        br_pool = pool_mixer(z[..., 0:W], pool_maps[l], pool_scale[l])
        br_fnet = fourier_mixer(z[..., W:2 * W])
        br_conf = conformer_conv(z[..., 2 * W:4 * W], conv_dw_w[l], conv_dw_b[l], conv_ln_g[l], conv_ln_b[l])
        br_sconv = short_gated_conv(z[..., 4 * W:5 * W], z[..., 5 * W:6 * W], z[..., 6 * W:7 * W], sconv_w[l])
        branches = (br_pool, br_fnet, br_conf, br_sconv)
        mix = jnp.zeros_like(x)
        for n in range(N_BRANCH):
            mix = mix + jax.nn.sigmoid(h @ w_gate[l, n]) * (branches[n] @ w_branch[l, n])
        x = x + gt1 * (mix @ w_out[l])
        h2 = rmsnorm(x, g_norm_ffn[l]) * (1.0 + sc2) + sh2
        x = x + gt2 * peer(h2, w_q[l], sub_keys[l], expert_u[l], expert_v[l])
    return rmsnorm(x, g_final)


def setup_inputs(seed: int = 0) -> dict:
    key = jax.random.key(seed)
    ks = jax.random.split(key, 24)
    f32 = jnp.float32
    D = D_MODEL
    nrm = lambda k, shape, s: jax.random.normal(k, shape, f32) * s
    return {
        "x_prompt": nrm(ks[0], (BATCH, SEQ, D), 1.0),
        "x_sample": nrm(ks[1], (DEC_BATCH, DEC_SEQ, D), 1.0),
        "c_prompt": nrm(ks[2], (BATCH, D), 1.0),
        "c_sample": nrm(ks[3], (DEC_BATCH, D), 1.0),
        "w_ada": nrm(ks[4], (DEPTH, D, N_MOD * D), 0.5 * D ** -0.5),
        "b_ada": nrm(ks[5], (DEPTH, N_MOD * D), 0.02),
        "g_norm_mix": 1.0 + nrm(ks[6], (DEPTH, D), 0.02),
        "w_in": nrm(ks[7], (DEPTH, D, IN_COLS), D ** -0.5),
        "pool_maps": nrm(ks[8], (DEPTH, POOL_GROUPS, POOL_GW, POOL_GW), POOL_GW ** -0.5),
        "pool_scale": 1.0 + nrm(ks[9], (DEPTH, W_BRANCH), 0.02),
        "conv_dw_w": nrm(ks[10], (DEPTH, CONV_WIDTH, W_BRANCH), CONV_WIDTH ** -0.5),
        "conv_dw_b": nrm(ks[11], (DEPTH, W_BRANCH), 0.02),
        "conv_ln_g": 1.0 + nrm(ks[12], (DEPTH, W_BRANCH), 0.02),
        "conv_ln_b": nrm(ks[13], (DEPTH, W_BRANCH), 0.02),
        "sconv_w": nrm(ks[14], (DEPTH, SCONV_WIDTH, W_BRANCH), SCONV_WIDTH ** -0.5),
        "w_branch": nrm(ks[15], (DEPTH, N_BRANCH, W_BRANCH, D), W_BRANCH ** -0.5),
        "w_gate": nrm(ks[16], (DEPTH, N_BRANCH, D, D), D ** -0.5),
        "w_out": nrm(ks[17], (DEPTH, D, D), D ** -0.5),
        "g_norm_ffn": 1.0 + nrm(ks[18], (DEPTH, D), 0.02),
        "w_q": nrm(ks[19], (DEPTH, D, PEER_HEADS * D_QUERY), D ** -0.5),
        "sub_keys": nrm(ks[20], (DEPTH, PEER_HEADS, 2, N_KEYS, D_HALF), D_HALF ** -0.5),
        "expert_u": nrm(ks[21], (DEPTH, N_EXPERTS, D), D ** -0.5),
        "expert_v": nrm(ks[22], (DEPTH, N_EXPERTS, D), 1.0),
        "g_final": 1.0 + nrm(ks[23], (D,), 0.02),
    }


def reference(x_prompt, x_sample, c_prompt, c_sample, w_ada, b_ada, g_norm_mix, w_in, pool_maps,
              pool_scale, conv_dw_w, conv_dw_b, conv_ln_g, conv_ln_b, sconv_w, w_branch, w_gate,
              w_out, g_norm_ffn, w_q, sub_keys, expert_u, expert_v, g_final):
    y_prompt = trunk(x_prompt, c_prompt, w_ada, b_ada, g_norm_mix, w_in, pool_maps, pool_scale,
                     conv_dw_w, conv_dw_b, conv_ln_g, conv_ln_b, sconv_w, w_branch, w_gate, w_out,
                     g_norm_ffn, w_q, sub_keys, expert_u, expert_v, g_final)
    y_sample = trunk(x_sample, c_sample, w_ada, b_ada, g_norm_mix, w_in, pool_maps, pool_scale,
                     conv_dw_w, conv_dw_b, conv_ln_g, conv_ln_b, sconv_w, w_branch, w_gate, w_out,
                     g_norm_ffn, w_q, sub_keys, expert_u, expert_v, g_final)
    return (y_prompt, y_sample)
```

```python
import functools
import math

import jax
import jax.numpy as jnp
from jax import lax
from jax.experimental import pallas as pl
from jax.experimental.pallas import tpu as pltpu

D_MODEL = 1024
N_MOD = 6
W_BRANCH = 512
N_BRANCH = 4
POOL_WINDOWS = (2, 4, 8, 16)
POOL_GW = 128
FNET_HW = 128
CONV_WIDTH = 31
SCONV_WIDTH = 3
IN_COLS = 7 * W_BRANCH
N_KEYS = 128
N_EXPERTS = N_KEYS * N_KEYS
PEER_HEADS = 8
PEER_TOPK = 16
D_HALF = 128
EPS = 1e-6

HALO = 16
FFT_LO = {16384: 128, 2048: 16}
EXPERT_BLOCK = 256
VMEM_LIMIT = 56 * 1024 * 1024

F32 = jnp.float32
BF16 = jnp.bfloat16
NEG_INF = float("-inf")


def _params(sem, vmem=VMEM_LIMIT):
    return pltpu.CompilerParams(dimension_semantics=sem, vmem_limit_bytes=vmem)


def _dot(a, b):
    return jnp.dot(a, b, preferred_element_type=F32)


def _dot_nt(a, b):
    return lax.dot_general(a, b, (((1,), (1,)), ((), ())), preferred_element_type=F32)


def _sigmoid(x):
    return 1.0 / (1.0 + jnp.exp(-x))


def _gelu_tanh(x):
    c = math.sqrt(2.0 / math.pi)
    return 0.5 * x * (1.0 + jnp.tanh(c * (x + 0.044715 * (x * x * x))))


def _mod_norm(x, g, shift, scale):
    r = lax.rsqrt(jnp.mean(x * x, axis=-1, keepdims=True) + EPS)
    return (x * r * g) * (1.0 + scale) + shift


def _mod_kernel(c_ref, w_ref, b_ref, o_ref):
    c = c_ref[...]
    a = c * _sigmoid(c)
    w = w_ref[0]
    a_hi = a.astype(BF16)
    a_lo = (a - a_hi.astype(F32)).astype(BF16)
    w_hi = w.astype(BF16)
    w_lo = (w - w_hi.astype(F32)).astype(BF16)
    o_ref[0] = _dot(a_hi, w_hi) + _dot(a_lo, w_hi) + _dot(a_hi, w_lo) + b_ref[0]


def _modulation(c_all, w_ada, b_ada):
    depth = w_ada.shape[0]
    nb = c_all.shape[0]
    cols = N_MOD * D_MODEL
    cb = 1536
    return pl.pallas_call(
        _mod_kernel,
        grid=(depth, cols // cb),
        in_specs=[
            pl.BlockSpec((nb, D_MODEL), lambda l, j: (0, 0)),
            pl.BlockSpec((1, D_MODEL, cb), lambda l, j: (l, 0, j)),
            pl.BlockSpec((1, 1, cb), lambda l, j: (l, 0, j)),
        ],
        out_specs=pl.BlockSpec((1, nb, cb), lambda l, j: (l, 0, j)),
        out_shape=jax.ShapeDtypeStruct((depth, nb, cols), F32),
        compiler_params=_params(("parallel", "parallel")),
        name="modulation",
    )(c_all, w_ada, b_ada.reshape(depth, 1, cols))


def _in_proj_kernel(x_ref, mod_ref, g_ref, w_ref, zp_ref, zf_ref, zc_ref, zs_ref):
    x = x_ref[0]
    h = _mod_norm(x, g_ref[...], mod_ref[0, 0:1, :], mod_ref[0, 1:2, :]).astype(BF16)
    W = W_BRANCH
    zp_ref[0] = _dot(h, w_ref[:, 0:W]).astype(BF16)
    zf_ref[0] = _dot(h, w_ref[:, W:2 * W]).astype(BF16)
    zc_ref[0] = _dot(h, w_ref[:, 2 * W:4 * W]).astype(BF16)
    zs_ref[0] = _dot(h, w_ref[:, 4 * W:7 * W]).astype(BF16)


def _in_proj(x, mod, g, w_in, tile=512):
    B, S, D = x.shape
    W = W_BRANCH
    tok = lambda c: pl.BlockSpec((1, tile, c), lambda b, i: (b, i, 0))
    return pl.pallas_call(
        _in_proj_kernel,
        grid=(B, S // tile),
        in_specs=[
            tok(D),
            pl.BlockSpec((1, N_MOD, D), lambda b, i: (b, 0, 0)),
            pl.BlockSpec((1, D), lambda b, i: (0, 0)),
            pl.BlockSpec((D, IN_COLS), lambda b, i: (0, 0)),
        ],
        out_specs=[tok(W), tok(W), tok(2 * W), tok(3 * W)],
        out_shape=[jax.ShapeDtypeStruct((B, S, c), BF16) for c in (W, W, 2 * W, 3 * W)],
        compiler_params=_params(("parallel", "parallel")),
        name="in_proj",
    )(x, mod, g.reshape(1, D), w_in)


def _dft_tables(S):
    n_lo = FFT_LO[S]
    n_hi = S // n_lo
    c = jnp.arange(FNET_HW, dtype=jnp.int32)
    ang = (2.0 * math.pi / FNET_HW) * ((c[:, None] * c[None, :]) % FNET_HW).astype(F32)
    chan = jnp.concatenate([jnp.cos(ang), -jnp.sin(ang)], axis=1).astype(BF16)
    s_lo = jnp.arange(n_lo, dtype=jnp.int32)[:, None, None]
    k_lo = jnp.arange(n_hi, dtype=jnp.int32)[None, :, None]
    s_hi = jnp.arange(n_hi, dtype=jnp.int32)[None, None, :]
    phase = ((s_hi * k_lo) % n_hi * n_lo + (s_lo * k_lo) % S) % S
    th = (2.0 * math.pi / S) * phase.astype(F32)
    stage1 = jnp.stack([jnp.cos(th), jnp.sin(th)], axis=1).astype(BF16)
    k_hi = jnp.arange(n_lo, dtype=jnp.int32)[:, None]
    s2 = jnp.arange(n_lo, dtype=jnp.int32)[None, :]
    ph = (2.0 * math.pi / n_lo) * ((k_hi * s2) % n_lo).astype(F32)
    stage2 = jnp.concatenate([jnp.cos(ph), jnp.sin(ph)], axis=1).astype(BF16)
    return chan, stage1, stage2


def _fft1_kernel(y_ref, chan_ref, m_ref, o_ref, *, group):
    W = W_BRANCH
    for gi in range(group):
        yr, yi = [], []
        for hd in range(W // FNET_HW):
            t = y_ref[0, :, gi * W + hd * FNET_HW: gi * W + (hd + 1) * FNET_HW]
            yc = _dot(t, chan_ref[...])
            yr.append(yc[:, :FNET_HW])
            yi.append(yc[:, FNET_HW:])
        yr = jnp.concatenate(yr, axis=1).astype(BF16)
        yi = jnp.concatenate(yi, axis=1).astype(BF16)
        mc = m_ref[gi, 0]
        ms = m_ref[gi, 1]
        o_ref[0, 0, gi] = (_dot(mc, yr) + _dot(ms, yi)).astype(BF16)
        o_ref[0, 1, gi] = (_dot(mc, yi) - _dot(ms, yr)).astype(BF16)


def _fft2_kernel(b_ref, w_ref, o_ref, *, scale):
    o_ref[0] = (_dot(w_ref[...], b_ref[0]) * scale).astype(BF16)


def _fourier_mixer(zf, tables):
    B, S, W = zf.shape
    chan, stage1, stage2 = tables
    n_lo = FFT_LO[S]
    n_hi = S // n_lo
    group = 8
    y = zf.reshape(B, n_hi, n_lo * W)
    bst = pl.pallas_call(
        functools.partial(_fft1_kernel, group=group),
        grid=(B, n_lo // group),
        in_specs=[
            pl.BlockSpec((1, n_hi, group * W), lambda b, j: (b, 0, j)),
            pl.BlockSpec((FNET_HW, 2 * FNET_HW), lambda b, j: (0, 0)),
            pl.BlockSpec((group, 2, n_hi, n_hi), lambda b, j: (j, 0, 0, 0)),
        ],
        out_specs=pl.BlockSpec((1, 2, group, n_hi, W), lambda b, j: (b, 0, j, 0, 0)),
        out_shape=jax.ShapeDtypeStruct((B, 2, n_lo, n_hi, W), BF16),
        compiler_params=_params(("parallel", "parallel")),
        name="fft_stage1",
    )(y, chan, stage1)
    lanes = 8192
    bview = bst.reshape(B, 2 * n_lo, n_hi * W)
    out = pl.pallas_call(
        functools.partial(_fft2_kernel, scale=1.0 / math.sqrt(S * FNET_HW)),
        grid=(B, n_hi * W // lanes),
        in_specs=[
            pl.BlockSpec((1, 2 * n_lo, lanes), lambda b, j: (b, 0, j)),
            pl.BlockSpec((n_lo, 2 * n_lo), lambda b, j: (0, 0)),
        ],
        out_specs=pl.BlockSpec((1, n_lo, lanes), lambda b, j: (b, 0, j)),
        out_shape=jax.ShapeDtypeStruct((B, n_lo, n_hi * W), BF16),
        compiler_params=_params(("parallel", "parallel")),
        name="fft_stage2",
    )(bview, stage2)
    return out.reshape(B, S, W)


def _local_mix_kernel(zp_ref, zp_prev, zp_next, zc_ref, zc_prev, zc_next, zs_ref, zs_prev, zs_next,
                      maps_ref, pscale_ref, dww_ref, dwb_ref, lng_ref, lnb_ref, scw_ref,
                      pool_ref, conf_ref, sconv_ref, u_ref, v_ref, *, tile, seq):
    W = W_BRANCH
    i = pl.program_id(1)
    has_prev = i > 0
    has_next = i < pl.num_programs(1) - 1
    ext = tile + 2 * HALO

    def extended(main, prev, nxt):
        p = prev[0].astype(F32)
        n = nxt[0].astype(F32)
        p = jnp.where(has_prev, p, 0.0)
        n = jnp.where(has_next, n, 0.0)
        return jnp.concatenate([p, main[0].astype(F32), n], axis=0)

    zp_ext = extended(zp_ref, zp_prev, zp_next).astype(BF16)
    row = lax.broadcasted_iota(jnp.int32, (tile, ext), 0)
    col = lax.broadcasted_iota(jnp.int32, (tile, ext), 1)
    tpos = i * tile + lax.broadcasted_iota(jnp.int32, (tile, POOL_GW), 0)
    for gi, w in enumerate(POOL_WINDOWS):
        half = w // 2
        band = ((col >= row + (HALO - half)) & (col < row + (HALO + half)))
        band = jnp.where(band, 1.0, 0.0).astype(BF16)
        zg = zp_ext[:, gi * POOL_GW:(gi + 1) * POOL_GW]
        win = _dot(band, zg)
        cnt = (jnp.minimum(tpos + half, seq) - jnp.maximum(tpos - half, 0)).astype(F32)
        p = win / cnt - zp_ref[0, :, gi * POOL_GW:(gi + 1) * POOL_GW].astype(F32)
        y = _dot(p.astype(BF16), maps_ref[gi])
        y = y * pscale_ref[:, gi * POOL_GW:(gi + 1) * POOL_GW]
        pool_ref[0, :, gi * POOL_GW:(gi + 1) * POOL_GW] = y.astype(BF16)

    zc_ext = extended(zc_ref, zc_prev, zc_next)
    u_ref[...] = zc_ext[:, :W] * _sigmoid(zc_ext[:, W:])
    pad = CONV_WIDTH // 2
    cols = []
    for cb in range(W // 128):
        cs = slice(cb * 128, (cb + 1) * 128)
        acc = jnp.zeros((tile, 128), F32)
        for k in range(CONV_WIDTH):
            acc = acc + dww_ref[k:k + 1, cs] * u_ref[pl.ds(HALO - pad + k, tile), cs]
        cols.append(acc + dwb_ref[:, cs])
    uc = jnp.concatenate(cols, axis=1)
    mu = jnp.mean(uc, axis=-1, keepdims=True)
    var = jnp.mean(jnp.square(uc - mu), axis=-1, keepdims=True)
    yn = (uc - mu) * lax.rsqrt(var + EPS) * lng_ref[...] + lnb_ref[...]
    conf_ref[0] = (yn * _sigmoid(yn)).astype(BF16)

    zs_ext = extended(zs_ref, zs_prev, zs_next)
    v_ref[...] = zs_ext[:, W:2 * W] * zs_ext[:, 2 * W:3 * W]
    pad = SCONV_WIDTH // 2
    acc = jnp.zeros((tile, W), F32)
    for k in range(SCONV_WIDTH):
        acc = acc + scw_ref[k:k + 1, :] * v_ref[pl.ds(HALO - pad + k, tile), :]
    sconv_ref[0] = (zs_ref[0, :, 0:W].astype(F32) * acc).astype(BF16)


def _local_mix(zp, zc, zs, pool_maps, pool_scale, dw_w, dw_b, ln_g, ln_b, sc_w, tile=256):
    B, S, W = zp.shape
    hb = tile // HALO
    nh = S // HALO

    def trio(c):
        return [
            pl.BlockSpec((1, tile, c), lambda b, i: (b, i, 0)),
            pl.BlockSpec((1, HALO, c), lambda b, i: (b, jnp.maximum(i * hb - 1, 0), 0)),
            pl.BlockSpec((1, HALO, c), lambda b, i: (b, jnp.minimum((i + 1) * hb, nh - 1), 0)),
        ]

    full = lambda a: pl.BlockSpec(a.shape, lambda b, i: (0,) * a.ndim)
    row = lambda v: v.reshape(1, W)
    consts = [pool_maps, row(pool_scale), dw_w, row(dw_b), row(ln_g), row(ln_b), sc_w]
    out = pl.BlockSpec((1, tile, W), lambda b, i: (b, i, 0))
    return pl.pallas_call(
        functools.partial(_local_mix_kernel, tile=tile, seq=S),
        grid=(B, S // tile),
        in_specs=trio(W) + trio(2 * W) + trio(3 * W) + [full(a) for a in consts],
        out_specs=[out, out, out],
        out_shape=[jax.ShapeDtypeStruct((B, S, W), BF16)] * 3,
        scratch_shapes=[pltpu.VMEM((tile + 2 * HALO, W), F32), pltpu.VMEM((tile + 2 * HALO, W), F32)],
        compiler_params=_params(("parallel", "parallel")),
        name="local_mix",
    )(zp, zp, zp, zc, zc, zc, zs, zs, zs, *consts)


def _merge_kernel(x_ref, mod_ref, g_ref, b0, b1, b2, b3, wg_ref, wb_ref, wo_ref, o_ref):
    x = x_ref[0]
    h = _mod_norm(x, g_ref[...], mod_ref[0, 0:1, :], mod_ref[0, 1:2, :]).astype(BF16)
    mix = None
    for n, br in enumerate((b0, b1, b2, b3)):
        t = _sigmoid(_dot(h, wg_ref[n])) * _dot(br[0], wb_ref[n])
        mix = t if mix is None else mix + t
    y = _dot(mix.astype(BF16), wo_ref[...])
    o_ref[0] = x + mod_ref[0, 2:3, :] * y


def _merge(x, mod, g, branches, w_gate, w_branch, w_out, tile=256):
    B, S, D = x.shape
    W = W_BRANCH
    tok = lambda c: pl.BlockSpec((1, tile, c), lambda b, i: (b, i, 0))
    return pl.pallas_call(
        _merge_kernel,
        grid=(B, S // tile),
        in_specs=[
            tok(D),
            pl.BlockSpec((1, N_MOD, D), lambda b, i: (b, 0, 0)),
            pl.BlockSpec((1, D), lambda b, i: (0, 0)),
            tok(W), tok(W), tok(W), tok(W),
            pl.BlockSpec((N_BRANCH, D, D), lambda b, i: (0, 0, 0)),
            pl.BlockSpec((N_BRANCH, W, D), lambda b, i: (0, 0, 0)),
            pl.BlockSpec((D, D), lambda b, i: (0, 0)),
        ],
        out_specs=tok(D),
        out_shape=jax.ShapeDtypeStruct((B, S, D), F32),
        compiler_params=_params(("parallel", "parallel")),
        name="merge",
    )(x, mod, g.reshape(1, D), *branches, w_gate, w_branch, w_out)


def _top_values(s, vals_ref, n):
    cur = s
    for r in range(n):
        m = jnp.max(cur, axis=0, keepdims=True)
        vals_ref[r:r + 1, :] = m
        cur = jnp.where(cur == m, NEG_INF, cur)


def _peer_prep_kernel(x_ref, mod_ref, g_ref, wq_ref, keys_ref,
                      h2_ref, thr_ref, s2_ref, p1_ref, p2_ref, s_scr, va_ref, vb_ref, *, tile):
    K = PEER_TOPK
    hd = pl.program_id(2)

    @pl.when(hd == 0)
    def _():
        h2_ref[0] = _mod_norm(x_ref[0], g_ref[...], mod_ref[0, 3:4, :], mod_ref[0, 4:5, :]).astype(BF16)

    q = _dot(h2_ref[0], wq_ref[...]).astype(BF16)
    s_scr[0] = _dot_nt(keys_ref[0, 0], q[:, :D_HALF])
    s_scr[1] = _dot_nt(keys_ref[0, 1], q[:, D_HALF:])

    rows16 = lax.broadcasted_iota(jnp.int32, (K, 128), 0)
    rows8 = lax.broadcasted_iota(jnp.int32, (8, 128), 0)
    for lt in range(tile // 128):
        ls = slice(lt * 128, (lt + 1) * 128)
        s1 = s_scr[0, :, ls]
        s2 = s_scr[1, :, ls]
        _top_values(s1, va_ref, K)
        _top_values(s2, vb_ref, K)
        a = va_ref[...]
        b = vb_ref[...]
        cands = [jnp.where(rows16 < K, a[0:1] + b, NEG_INF)]
        for ia in range(1, 8):
            cands.append(jnp.where(rows8 < K // (ia + 1), a[ia:ia + 1] + b[0:8], NEG_INF))
        cands.append(a[8:16] + b[0:1])
        cur = list(cands)
        kth = None
        for r in range(K + 1):
            m = jnp.max(cur[0], axis=0, keepdims=True)
            for c in cur[1:]:
                m = jnp.maximum(m, jnp.max(c, axis=0, keepdims=True))
            if r == K - 1:
                kth = m
            if r < K:
                cur = [jnp.where(c == m, NEG_INF, c) for c in cur]
        tau = 0.5 * (kth + m)
        top = a[0:1] + b[0:1]
        z = None
        for c in cands:
            e = jnp.sum(jnp.where(c >= tau, jnp.exp(c - top), 0.0), axis=0, keepdims=True)
            z = e if z is None else z + e
        thr_ref[0, 0, :, ls] = tau - s1
        s2_ref[0, 0, :, ls] = s2
        p1_ref[0, 0, :, ls] = jnp.exp(s1 - a[0:1])
        p2_ref[0, 0, :, ls] = jnp.exp(s2 - b[0:1]) / z


def _peer_prep(x, mod, g, w_q, keys, tile=512):
    B, S, D = x.shape
    H = PEER_HEADS
    sc = pl.BlockSpec((1, 1, N_KEYS, tile), lambda b, i, h: (b, h, 0, i))
    sc_shape = jax.ShapeDtypeStruct((B, H, N_KEYS, S), F32)
    return pl.pallas_call(
        functools.partial(_peer_prep_kernel, tile=tile),
        grid=(B, S // tile, H),
        in_specs=[
            pl.BlockSpec((1, tile, D), lambda b, i, h: (b, i, 0)),
            pl.BlockSpec((1, N_MOD, D), lambda b, i, h: (b, 0, 0)),
            pl.BlockSpec((1, D), lambda b, i, h: (0, 0)),
            pl.BlockSpec((D, 2 * D_HALF), lambda b, i, h: (0, h)),
            pl.BlockSpec((1, 2, N_KEYS, D_HALF), lambda b, i, h: (h, 0, 0, 0)),
        ],
        out_specs=[pl.BlockSpec((1, tile, D), lambda b, i, h: (b, i, 0)), sc, sc, sc, sc],
        out_shape=[jax.ShapeDtypeStruct((B, S, D), BF16), sc_shape, sc_shape, sc_shape, sc_shape],
        scratch_shapes=[
            pltpu.VMEM((2, N_KEYS, tile), F32),
            pltpu.VMEM((PEER_TOPK, 128), F32),
            pltpu.VMEM((PEER_TOPK, 128), F32),
        ],
        compiler_params=_params(("parallel", "parallel", "arbitrary")),
        name="peer_prep",
    )(x, mod, g.reshape(1, D), w_q, keys)


def _peer_dense_kernel(h2_ref, thr_ref, s2_ref, p1_ref, p2_ref, u_ref, vt_ref, x_ref, mod_ref, gf_ref,
                       o_ref, acc_ref, *, tile, blocks, final_norm):
    ci = pl.program_id(2)
    per_block = EXPERT_BLOCK // N_KEYS
    RB = 64

    @pl.when(ci == 0)
    def _():
        acc_ref[...] = jnp.zeros_like(acc_ref)

    h2 = h2_ref[0]
    assert blocks * per_block == 8
    i_base = pl.multiple_of(ci * 8, 8)

    for bi in range(blocks):
        a = _dot_nt(u_ref[bi], h2)
        parts = []
        for ii in range(per_block):
            r = bi * per_block + ii
            for rb in range(N_KEYS // RB):
                rs = slice(rb * RB, (rb + 1) * RB)
                lanes = []
                for lt in range(tile // 128):
                    ls = slice(lt * 128, (lt + 1) * 128)
                    gate = None
                    for h in range(PEER_HEADS):
                        thr = thr_ref[0, h, pl.ds(i_base, 8), ls][r:r + 1]
                        p1 = p1_ref[0, h, pl.ds(i_base, 8), ls][r:r + 1]
                        w = jnp.where(s2_ref[0, h, rs, ls] >= thr, p2_ref[0, h, rs, ls] * p1, 0.0)
                        gate = w if gate is None else gate + w
                    act = a[ii * N_KEYS + rb * RB: ii * N_KEYS + (rb + 1) * RB, ls]
                    lanes.append((_gelu_tanh(act) * gate).astype(BF16))
                parts.append(jnp.concatenate(lanes, axis=1))
        hid = jnp.concatenate(parts, axis=0)
        acc_ref[...] += _dot(vt_ref[bi], hid)

    @pl.when(ci == pl.num_programs(2) - 1)
    def _():
        y = x_ref[0] + mod_ref[0, 5:6, :] * acc_ref[...].T
        if final_norm:
            r = lax.rsqrt(jnp.mean(y * y, axis=-1, keepdims=True) + EPS)
            y = y * r * gf_ref[...]
        o_ref[0] = y


def _peer_dense(h2, thr, s2, p1, p2, u_blocks, vt_blocks, x, mod, g_final, final_norm, tile=512, blocks=4):
    B, S, D = x.shape
    H = PEER_HEADS
    n_chunks = (N_EXPERTS // EXPERT_BLOCK) // blocks
    sc = pl.BlockSpec((1, H, N_KEYS, tile), lambda b, i, c: (b, 0, 0, i))
    tok = pl.BlockSpec((1, tile, D), lambda b, i, c: (b, i, 0))
    return pl.pallas_call(
        functools.partial(_peer_dense_kernel, tile=tile, blocks=blocks, final_norm=final_norm),
        grid=(B, S // tile, n_chunks),
        in_specs=[
            tok, sc, sc, sc, sc,
            pl.BlockSpec((blocks, EXPERT_BLOCK, D), lambda b, i, c: (c, 0, 0)),
            pl.BlockSpec((blocks, D, EXPERT_BLOCK), lambda b, i, c: (c, 0, 0)),
            tok,
            pl.BlockSpec((1, N_MOD, D), lambda b, i, c: (b, 0, 0)),
            pl.BlockSpec((1, D), lambda b, i, c: (0, 0)),
        ],
        out_specs=tok,
        out_shape=jax.ShapeDtypeStruct((B, S, D), F32),
        scratch_shapes=[pltpu.VMEM((D, tile), F32)],
        compiler_params=_params(("parallel", "parallel", "arbitrary")),
        name="peer_dense",
    )(h2, thr, s2, p1, p2, u_blocks, vt_blocks, x, mod, g_final.reshape(1, D))


def _trunk(x, mod_all, layers, g_final, tables):
    depth = len(layers)
    for l, p in enumerate(layers):
        mod = mod_all[l]
        zp, zf, zc, zs = _in_proj(x, mod, p["g_mix"], p["w_in"])
        br_fnet = _fourier_mixer(zf, tables)
        br_pool, br_conf, br_sconv = _local_mix(zp, zc, zs, p["pool_maps"], p["pool_scale"], p["dw_w"],
                                                p["dw_b"], p["ln_g"], p["ln_b"], p["sc_w"])
        x = _merge(x, mod, p["g_mix"], (br_pool, br_fnet, br_conf, br_sconv),
                   p["w_gate"], p["w_branch"], p["w_out"])
        h2, thr, s2, p1, p2 = _peer_prep(x, mod, p["g_ffn"], p["w_q"], p["keys"])
        x = _peer_dense(h2, thr, s2, p1, p2, p["u"], p["vt"], x, mod, g_final, l == depth - 1)
    return x


def kernel(x_prompt, x_sample, c_prompt, c_sample, w_ada, b_ada, g_norm_mix, w_in, pool_maps, pool_scale,
           conv_dw_w, conv_dw_b, conv_ln_g, conv_ln_b, sconv_w, w_branch, w_gate, w_out, g_norm_ffn, w_q,
           sub_keys, expert_u, expert_v, g_final):
    depth = w_in.shape[0]
    nbp = x_prompt.shape[0]
    c_all = jnp.concatenate([c_prompt, c_sample], axis=0)
    mod_all = _modulation(c_all, w_ada, b_ada).reshape(depth, -1, N_MOD, D_MODEL)
    nblk = N_EXPERTS // EXPERT_BLOCK
    layers = []
    for l in range(depth):
        layers.append(dict(
            g_mix=g_norm_mix[l], g_ffn=g_norm_ffn[l],
            w_in=w_in[l].astype(BF16), pool_maps=pool_maps[l].astype(BF16), pool_scale=pool_scale[l],
            dw_w=conv_dw_w[l], dw_b=conv_dw_b[l], ln_g=conv_ln_g[l], ln_b=conv_ln_b[l], sc_w=sconv_w[l],
            w_gate=w_gate[l].astype(BF16), w_branch=w_branch[l].astype(BF16), w_out=w_out[l].astype(BF16),
            w_q=w_q[l].astype(BF16), keys=sub_keys[l].astype(BF16),
            u=expert_u[l].astype(BF16).reshape(nblk, EXPERT_BLOCK, D_MODEL),
            vt=jnp.swapaxes(expert_v[l].astype(BF16).reshape(nblk, EXPERT_BLOCK, D_MODEL), 1, 2),
        ))
    outs = []
    for x, mod in ((x_prompt, mod_all[:, :nbp]), (x_sample, mod_all[:, nbp:])):
        outs.append(_trunk(x, mod, layers, g_final, _dft_tables(x.shape[1])))
    return tuple(outs)
```

```python
import functools
import math

import jax
import jax.numpy as jnp
from jax import lax
from jax.experimental import pallas as pl
from jax.experimental.pallas import tpu as pltpu

D_MODEL = 1024
N_MOD = 6
W_BRANCH = 512
N_BRANCH = 4
POOL_WINDOWS = (2, 4, 8, 16)
POOL_GW = 128
FNET_HW = 128
CONV_WIDTH = 31
SCONV_WIDTH = 3
IN_COLS = 7 * W_BRANCH
N_KEYS = 128
N_EXPERTS = N_KEYS * N_KEYS
PEER_HEADS = 8
PEER_TOPK = 16
D_HALF = 128
EPS = 1e-6

SUBLANES = 8
HALO = 16
FFT_LO = {16384: 128, 2048: 16}
EXPERT_BLOCK = 256
CHUNK_KEYS = 8
VMEM_LIMIT = 56 * 1024 * 1024

F32 = jnp.float32
BF16 = jnp.bfloat16
NEG_INF = float("-inf")


def _params(sem, vmem=VMEM_LIMIT):
    return pltpu.CompilerParams(dimension_semantics=sem, vmem_limit_bytes=vmem)


def _dot(a, b):
    return jnp.dot(a, b, preferred_element_type=F32)


def _dot_nt(a, b):
    return lax.dot_general(a, b, (((1,), (1,)), ((), ())), preferred_element_type=F32)


def _sigmoid(x):
    return 1.0 / (1.0 + jnp.exp(-x))


def _gelu_tanh(x):
    c = math.sqrt(2.0 / math.pi)
    return 0.5 * x * (1.0 + jnp.tanh(c * (x + 0.044715 * (x * x * x))))


def _mod_norm(x, g, shift, scale):
    r = lax.rsqrt(jnp.mean(x * x, axis=-1, keepdims=True) + EPS)
    return (x * r * g) * (1.0 + scale) + shift


def _mod_kernel(c_ref, w_ref, b_ref, o_ref):
    c = c_ref[...]
    a = c * _sigmoid(c)
    w = w_ref[0]
    a_hi = a.astype(BF16)
    a_lo = (a - a_hi.astype(F32)).astype(BF16)
    w_hi = w.astype(BF16)
    w_lo = (w - w_hi.astype(F32)).astype(BF16)
    o_ref[0] = _dot(a_hi, w_hi) + _dot(a_lo, w_hi) + _dot(a_hi, w_lo) + b_ref[0]


def _modulation(c_all, w_ada, b_ada):
    depth = w_ada.shape[0]
    nb = c_all.shape[0]
    cols = N_MOD * D_MODEL
    cb = 1536
    return pl.pallas_call(
        _mod_kernel,
        grid=(depth, cols // cb),
        in_specs=[
            pl.BlockSpec((nb, D_MODEL), lambda l, j: (0, 0)),
            pl.BlockSpec((1, D_MODEL, cb), lambda l, j: (l, 0, j)),
            pl.BlockSpec((1, 1, cb), lambda l, j: (l, 0, j)),
        ],
        out_specs=pl.BlockSpec((1, nb, cb), lambda l, j: (l, 0, j)),
        out_shape=jax.ShapeDtypeStruct((depth, nb, cols), F32),
        compiler_params=_params(("parallel", "parallel")),
        name="modulation",
    )(c_all, w_ada, b_ada.reshape(depth, 1, cols))


def _in_proj_kernel(x_ref, mod_ref, g_ref, w_ref, zp_ref, zf_ref, zc_ref, zs_ref):
    x = x_ref[0]
    h = _mod_norm(x, g_ref[...], mod_ref[0, 0:1, :], mod_ref[0, 1:2, :]).astype(BF16)
    W = W_BRANCH
    zp_ref[0] = _dot(h, w_ref[:, 0:W]).astype(BF16)
    zf_ref[0] = _dot(h, w_ref[:, W:2 * W]).astype(BF16)
    zc_ref[0] = _dot(h, w_ref[:, 2 * W:4 * W]).astype(BF16)
    zs_ref[0] = _dot(h, w_ref[:, 4 * W:7 * W]).astype(BF16)


def _in_proj(x, mod, g, w_in, tile=512):
    B, S, D = x.shape
    W = W_BRANCH
    tok = lambda c: pl.BlockSpec((1, tile, c), lambda b, i: (b, i, 0))
    return pl.pallas_call(
        _in_proj_kernel,
        grid=(B, S // tile),
        in_specs=[
            tok(D),
            pl.BlockSpec((1, N_MOD, D), lambda b, i: (b, 0, 0)),
            pl.BlockSpec((1, D), lambda b, i: (0, 0)),
            pl.BlockSpec((D, IN_COLS), lambda b, i: (0, 0)),
        ],
        out_specs=[tok(W), tok(W), tok(2 * W), tok(3 * W)],
        out_shape=[jax.ShapeDtypeStruct((B, S, c), BF16) for c in (W, W, 2 * W, 3 * W)],
        compiler_params=_params(("parallel", "parallel")),
        name="in_proj",
    )(x, mod, g.reshape(1, D), w_in)


def _dft_tables(S):
    n_lo = FFT_LO[S]
    n_hi = S // n_lo
    c = jnp.arange(FNET_HW, dtype=jnp.int32)
    ang = (2.0 * math.pi / FNET_HW) * ((c[:, None] * c[None, :]) % FNET_HW).astype(F32)
    chan = jnp.concatenate([jnp.cos(ang), -jnp.sin(ang)], axis=1).astype(BF16)
    s_lo = jnp.arange(n_lo, dtype=jnp.int32)[:, None, None]
    k_lo = jnp.arange(n_hi, dtype=jnp.int32)[None, :, None]
    s_hi = jnp.arange(n_hi, dtype=jnp.int32)[None, None, :]
    phase = ((s_hi * k_lo) % n_hi * n_lo + (s_lo * k_lo) % S) % S
    th = (2.0 * math.pi / S) * phase.astype(F32)
    stage1 = jnp.stack([jnp.cos(th), jnp.sin(th)], axis=1).astype(BF16)
    k_hi = jnp.arange(n_lo, dtype=jnp.int32)[:, None]
    s2 = jnp.arange(n_lo, dtype=jnp.int32)[None, :]
    ph = (2.0 * math.pi / n_lo) * ((k_hi * s2) % n_lo).astype(F32)
    stage2 = jnp.concatenate([jnp.cos(ph), jnp.sin(ph)], axis=1).astype(BF16)
    return chan, stage1, stage2


def _fft1_kernel(y_ref, chan_ref, m_ref, o_ref, *, group):
    W = W_BRANCH
    for gi in range(group):
        yr, yi = [], []
        for hd in range(W // FNET_HW):
            t = y_ref[0, :, gi * W + hd * FNET_HW: gi * W + (hd + 1) * FNET_HW]
            yc = _dot(t, chan_ref[...])
            yr.append(yc[:, :FNET_HW])
            yi.append(yc[:, FNET_HW:])
        yr = jnp.concatenate(yr, axis=1).astype(BF16)
        yi = jnp.concatenate(yi, axis=1).astype(BF16)
        mc = m_ref[gi, 0]
        ms = m_ref[gi, 1]
        o_ref[0, 0, gi] = (_dot(mc, yr) + _dot(ms, yi)).astype(BF16)
        o_ref[0, 1, gi] = (_dot(mc, yi) - _dot(ms, yr)).astype(BF16)


def _fft2_kernel(b_ref, w_ref, o_ref, *, scale):
    o_ref[0] = (_dot(w_ref[...], b_ref[0]) * scale).astype(BF16)


def _fourier_mixer(zf, tables):
    B, S, W = zf.shape
    chan, stage1, stage2 = tables
    n_lo = FFT_LO[S]
    n_hi = S // n_lo
    group = 8
    y = zf.reshape(B, n_hi, n_lo * W)
    bst = pl.pallas_call(
        functools.partial(_fft1_kernel, group=group),
        grid=(B, n_lo // group),
        in_specs=[
            pl.BlockSpec((1, n_hi, group * W), lambda b, j: (b, 0, j)),
            pl.BlockSpec((FNET_HW, 2 * FNET_HW), lambda b, j: (0, 0)),
            pl.BlockSpec((group, 2, n_hi, n_hi), lambda b, j: (j, 0, 0, 0)),
        ],
        out_specs=pl.BlockSpec((1, 2, group, n_hi, W), lambda b, j: (b, 0, j, 0, 0)),
        out_shape=jax.ShapeDtypeStruct((B, 2, n_lo, n_hi, W), BF16),
        compiler_params=_params(("parallel", "parallel")),
        name="fft_stage1",
    )(y, chan, stage1)
    lanes = 8192
    bview = bst.reshape(B, 2 * n_lo, n_hi * W)
    out = pl.pallas_call(
        functools.partial(_fft2_kernel, scale=1.0 / math.sqrt(S * FNET_HW)),
        grid=(B, n_hi * W // lanes),
        in_specs=[
            pl.BlockSpec((1, 2 * n_lo, lanes), lambda b, j: (b, 0, j)),
            pl.BlockSpec((n_lo, 2 * n_lo), lambda b, j: (0, 0)),
        ],
        out_specs=pl.BlockSpec((1, n_lo, lanes), lambda b, j: (b, 0, j)),
        out_shape=jax.ShapeDtypeStruct((B, n_lo, n_hi * W), BF16),
        compiler_params=_params(("parallel", "parallel")),
        name="fft_stage2",
    )(bview, stage2)
    return out.reshape(B, S, W)


def _local_mix_kernel(zp_ref, zp_prev, zp_next, zc_ref, zc_prev, zc_next, zs_ref, zs_prev, zs_next,
                      maps_ref, pscale_ref, dww_ref, dwb_ref, lng_ref, lnb_ref, scw_ref,
                      pool_ref, conf_ref, sconv_ref, u_ref, v_ref, *, tile, seq):
    W = W_BRANCH
    i = pl.program_id(1)
    has_prev = i > 0
    has_next = i < pl.num_programs(1) - 1
    ext = tile + 2 * HALO

    def extended(main, prev, nxt):
        p = prev[0].astype(F32)
        n = nxt[0].astype(F32)
        p = jnp.where(has_prev, p, 0.0)
        n = jnp.where(has_next, n, 0.0)
        return jnp.concatenate([p, main[0].astype(F32), n], axis=0)

    zp_ext = extended(zp_ref, zp_prev, zp_next).astype(BF16)
    row = lax.broadcasted_iota(jnp.int32, (tile, ext), 0)
    col = lax.broadcasted_iota(jnp.int32, (tile, ext), 1)
    tpos = i * tile + lax.broadcasted_iota(jnp.int32, (tile, POOL_GW), 0)
    for gi, w in enumerate(POOL_WINDOWS):
        half = w // 2
        band = ((col >= row + (HALO - half)) & (col < row + (HALO + half)))
        band = jnp.where(band, 1.0, 0.0).astype(BF16)
        zg = zp_ext[:, gi * POOL_GW:(gi + 1) * POOL_GW]
        win = _dot(band, zg)
        cnt = (jnp.minimum(tpos + half, seq) - jnp.maximum(tpos - half, 0)).astype(F32)
        p = win / cnt - zp_ref[0, :, gi * POOL_GW:(gi + 1) * POOL_GW].astype(F32)
        y = _dot(p.astype(BF16), maps_ref[gi])
        y = y * pscale_ref[:, gi * POOL_GW:(gi + 1) * POOL_GW]
        pool_ref[0, :, gi * POOL_GW:(gi + 1) * POOL_GW] = y.astype(BF16)

    zc_ext = extended(zc_ref, zc_prev, zc_next)
    u_ref[...] = zc_ext[:, :W] * _sigmoid(zc_ext[:, W:])
    pad = CONV_WIDTH // 2
    cols = []
    for cb in range(W // 128):
        cs = slice(cb * 128, (cb + 1) * 128)
        acc = jnp.zeros((tile, 128), F32)
        for k in range(CONV_WIDTH):
            acc = acc + dww_ref[k:k + 1, cs] * u_ref[pl.ds(HALO - pad + k, tile), cs]
        cols.append(acc + dwb_ref[:, cs])
    uc = jnp.concatenate(cols, axis=1)
    mu = jnp.mean(uc, axis=-1, keepdims=True)
    var = jnp.mean(jnp.square(uc - mu), axis=-1, keepdims=True)
    yn = (uc - mu) * lax.rsqrt(var + EPS) * lng_ref[...] + lnb_ref[...]
    conf_ref[0] = (yn * _sigmoid(yn)).astype(BF16)

    zs_ext = extended(zs_ref, zs_prev, zs_next)
    v_ref[...] = zs_ext[:, W:2 * W] * zs_ext[:, 2 * W:3 * W]
    pad = SCONV_WIDTH // 2
    acc = jnp.zeros((tile, W), F32)
    for k in range(SCONV_WIDTH):
        acc = acc + scw_ref[k:k + 1, :] * v_ref[pl.ds(HALO - pad + k, tile), :]
    sconv_ref[0] = (zs_ref[0, :, 0:W].astype(F32) * acc).astype(BF16)


def _local_mix(zp, zc, zs, pool_maps, pool_scale, dw_w, dw_b, ln_g, ln_b, sc_w, tile=256):
    B, S, W = zp.shape
    hb = tile // HALO
    nh = S // HALO

    def trio(c):
        return [
            pl.BlockSpec((1, tile, c), lambda b, i: (b, i, 0)),
            pl.BlockSpec((1, HALO, c), lambda b, i: (b, jnp.maximum(i * hb - 1, 0), 0)),
            pl.BlockSpec((1, HALO, c), lambda b, i: (b, jnp.minimum((i + 1) * hb, nh - 1), 0)),
        ]

    full = lambda a: pl.BlockSpec(a.shape, lambda b, i: (0,) * a.ndim)
    row = lambda v: v.reshape(1, W)
    consts = [pool_maps, row(pool_scale), dw_w, row(dw_b), row(ln_g), row(ln_b), sc_w]
    out = pl.BlockSpec((1, tile, W), lambda b, i: (b, i, 0))
    return pl.pallas_call(
        functools.partial(_local_mix_kernel, tile=tile, seq=S),
        grid=(B, S // tile),
        in_specs=trio(W) + trio(2 * W) + trio(3 * W) + [full(a) for a in consts],
        out_specs=[out, out, out],
        out_shape=[jax.ShapeDtypeStruct((B, S, W), BF16)] * 3,
        scratch_shapes=[pltpu.VMEM((tile + 2 * HALO, W), F32), pltpu.VMEM((tile + 2 * HALO, W), F32)],
        compiler_params=_params(("parallel", "parallel")),
        name="local_mix",
    )(zp, zp, zp, zc, zc, zc, zs, zs, zs, *consts)


def _merge_kernel(x_ref, mod_ref, g_ref, b0, b1, b2, b3, wg_ref, wb_ref, wo_ref, o_ref):
    x = x_ref[0]
    h = _mod_norm(x, g_ref[...], mod_ref[0, 0:1, :], mod_ref[0, 1:2, :]).astype(BF16)
    mix = None
    for n, br in enumerate((b0, b1, b2, b3)):
        t = _sigmoid(_dot(h, wg_ref[n])) * _dot(br[0], wb_ref[n])
        mix = t if mix is None else mix + t
    y = _dot(mix.astype(BF16), wo_ref[...])
    o_ref[0] = x + mod_ref[0, 2:3, :] * y


def _merge(x, mod, g, branches, w_gate, w_branch, w_out, tile=256):
    B, S, D = x.shape
    W = W_BRANCH
    tok = lambda c: pl.BlockSpec((1, tile, c), lambda b, i: (b, i, 0))
    return pl.pallas_call(
        _merge_kernel,
        grid=(B, S // tile),
        in_specs=[
            tok(D),
            pl.BlockSpec((1, N_MOD, D), lambda b, i: (b, 0, 0)),
            pl.BlockSpec((1, D), lambda b, i: (0, 0)),
            tok(W), tok(W), tok(W), tok(W),
            pl.BlockSpec((N_BRANCH, D, D), lambda b, i: (0, 0, 0)),
            pl.BlockSpec((N_BRANCH, W, D), lambda b, i: (0, 0, 0)),
            pl.BlockSpec((D, D), lambda b, i: (0, 0)),
        ],
        out_specs=tok(D),
        out_shape=jax.ShapeDtypeStruct((B, S, D), F32),
        compiler_params=_params(("parallel", "parallel")),
        name="merge",
    )(x, mod, g.reshape(1, D), *branches, w_gate, w_branch, w_out)


def _batcher_pairs(n):
    def merge(lo, hi, r):
        step = r * 2
        if step < hi - lo:
            yield from merge(lo, hi, step)
            yield from merge(lo + r, hi, step)
            for i in range(lo + r, hi - r, step):
                yield (i, i + r)
        else:
            yield (lo, lo + r)

    def sort(lo, hi):
        if hi - lo >= 1:
            mid = lo + (hi - lo) // 2
            yield from sort(lo, mid)
            yield from sort(mid + 1, hi)
            yield from merge(lo, hi, 1)

    return tuple(sort(0, n - 1))


_SORT_PAIRS = _batcher_pairs(PEER_TOPK)


def _exchange(xs, i, j):
    hi = jnp.maximum(xs[i], xs[j])
    lo = jnp.minimum(xs[i], xs[j])
    xs[i], xs[j] = hi, lo


def _top_sorted(xs):
    xs = list(xs)
    for i, j in _SORT_PAIRS:
        _exchange(xs, i, j)
    n = len(xs)
    for shift in (4, 2, 1):
        ys = [pltpu.roll(x, shift, axis=0) for x in xs]
        xs = [jnp.maximum(xs[k], ys[n - 1 - k]) for k in range(n)]
        stride = n // 2
        while stride >= 1:
            for k in range(n):
                if k & stride == 0:
                    _exchange(xs, k, k + stride)
            stride //= 2
    return xs


def _count_sorted(b, x, strict):
    test = (lambda p: p > x) if strict else (lambda p: p >= x)

    def pick(masks, lo, step):
        if not masks:
            return b[lo]
        return jnp.where(masks[0], pick(masks[1:], lo + step, step // 2), pick(masks[1:], lo, step // 2))

    masks = []
    count = None
    for weight, first in ((8, 7), (4, 3), (2, 1), (1, 0)):
        m = test(pick(masks, first, 8))
        masks.append(m)
        term = jnp.where(m, float(weight), 0.0)
        count = term if count is None else count + term
    return count + jnp.where(test(b[PEER_TOPK - 1]), 1.0, 0.0)


def _pack_rows(vals, rows):
    out = vals[0]
    for s in range(1, len(vals)):
        out = jnp.where(rows == s, vals[s], out)
    return out


def _bf16_bits(x):
    return pltpu.bitcast(x.astype(BF16).astype(F32), jnp.uint32)


def _bf16_pairs(x):
    bits = _bf16_bits(x)
    return pltpu.bitcast(bits | (bits >> 16), F32)


def _bf16_words(even, odd):
    return pltpu.bitcast((_bf16_bits(even) >> 16) | _bf16_bits(odd), F32)


def _peer_prep_kernel(x_ref, mod_ref, g_ref, wq_ref, keys_ref,
                      h2t_ref, r2_ref, q2_ref, c_ref, p1_ref, s_scr, h2_scr, *, tile):
    K = PEER_TOPK
    NV = N_KEYS // SUBLANES
    hd = pl.program_id(2)

    @pl.when(hd == 0)
    def _():
        h2 = _mod_norm(x_ref[0], g_ref[...], mod_ref[0, 3:4, :], mod_ref[0, 4:5, :])
        h2_scr[...] = h2.astype(BF16)
        h2t_ref[0] = h2.T.astype(BF16)

    q = _dot(h2_scr[...], wq_ref[...]).astype(BF16)
    s_scr[0] = _dot_nt(keys_ref[0, 0], q[:, :D_HALF])
    s_scr[1] = _dot_nt(keys_ref[0, 1], q[:, D_HALF:])

    rows = lax.broadcasted_iota(jnp.int32, (SUBLANES, 128), 0)
    for lt in range(tile // 128):
        ls = slice(lt * 128, (lt + 1) * 128)
        s1 = [s_scr[0, v * SUBLANES:(v + 1) * SUBLANES, ls] for v in range(NV)]
        s2 = [s_scr[1, v * SUBLANES:(v + 1) * SUBLANES, ls] for v in range(NV)]
        a = _top_sorted(s1)
        b = _top_sorted(s2)
        b_lo = _pack_rows(b[:SUBLANES], rows)
        b_hi = _pack_rows(b[SUBLANES:], rows)
        a_hi = _pack_rows(a[SUBLANES:], rows)
        cands = [a[0] + b_lo, a[0] + b_hi]
        for ia in range(1, SUBLANES):
            cands.append(jnp.where(rows < K // (ia + 1), a[ia] + b_lo, NEG_INF))
        cands.append(a_hi + b[0])
        cur = list(cands)
        kth = None
        for r in range(K + 1):
            m = cur[0]
            for c in cur[1:]:
                m = jnp.maximum(m, c)
            m = jnp.max(m, axis=0, keepdims=True)
            if r == K - 1:
                kth = m
            if r < K:
                cur = [jnp.where(c == m, NEG_INF, c) for c in cur]
        tau = 0.5 * (kth + m)
        top = a[0] + b[0]
        z = None
        for c in cands:
            e = jnp.where(c >= tau, jnp.exp(c - top), 0.0)
            z = e if z is None else z + e
        inv_z = 1.0 / jnp.sum(z, axis=0, keepdims=True)
        r2_parts, q2_parts = [], []
        for v in range(NV):
            rank = _count_sorted(b, s2[v], strict=True)
            cnt = _count_sorted(b, tau - s1[v], strict=False)
            r2_parts.append(rank)
            q2_parts.append(jnp.exp(s2[v] - b[0]) * inv_z)
            c_ref[0, 0, v, :, ls] = _bf16_pairs(cnt)
            p1_ref[0, 0, v, :, ls] = _bf16_pairs(jnp.exp(s1[v] - a[0]))
        for g in range(NV // 2):
            gs = slice(g * SUBLANES, (g + 1) * SUBLANES)
            r2_ref[0, 0, gs, ls] = _bf16_words(r2_parts[2 * g], r2_parts[2 * g + 1])
            q2_ref[0, 0, gs, ls] = _bf16_words(q2_parts[2 * g], q2_parts[2 * g + 1])


def _peer_prep(x, mod, g, w_q, keys, tile=512):
    B, S, D = x.shape
    H = PEER_HEADS
    NV = N_KEYS // SUBLANES
    col = pl.BlockSpec((1, 1, N_KEYS // 2, tile), lambda b, i, h: (b, h, 0, i))
    col_shape = jax.ShapeDtypeStruct((B, H, N_KEYS // 2, S), F32)
    rowb = pl.BlockSpec((1, 1, NV, SUBLANES, tile), lambda b, i, h: (b, h, 0, 0, i))
    rowb_shape = jax.ShapeDtypeStruct((B, H, NV, SUBLANES, S), F32)
    return pl.pallas_call(
        functools.partial(_peer_prep_kernel, tile=tile),
        grid=(B, S // tile, H),
        in_specs=[
            pl.BlockSpec((1, tile, D), lambda b, i, h: (b, i, 0)),
            pl.BlockSpec((1, N_MOD, D), lambda b, i, h: (b, 0, 0)),
            pl.BlockSpec((1, D), lambda b, i, h: (0, 0)),
            pl.BlockSpec((D, 2 * D_HALF), lambda b, i, h: (0, h)),
            pl.BlockSpec((1, 2, N_KEYS, D_HALF), lambda b, i, h: (h, 0, 0, 0)),
        ],
        out_specs=[pl.BlockSpec((1, D, tile), lambda b, i, h: (b, 0, i)), col, col, rowb, rowb],
        out_shape=[jax.ShapeDtypeStruct((B, D, S), BF16), col_shape, col_shape, rowb_shape, rowb_shape],
        scratch_shapes=[pltpu.VMEM((2, N_KEYS, tile), F32), pltpu.VMEM((tile, D), BF16)],
        compiler_params=_params(("parallel", "parallel", "arbitrary")),
        name="peer_prep",
    )(x, mod, g.reshape(1, D), w_q, keys)


def _peer_dense_kernel(h2t_ref, r2_ref, q2_ref, c_ref, p1_ref, u_ref, vt_ref, x_ref, mod_ref, gf_ref,
                       o_ref, acc_ref, a_ref, hid_ref, *, tile, final_norm):
    ci = pl.program_id(2)
    blocks = CHUNK_KEYS * N_KEYS // EXPERT_BLOCK
    per_block = EXPERT_BLOCK // N_KEYS
    PR = 2 * SUBLANES

    @pl.when(ci == 0)
    def _():
        acc_ref[...] = jnp.zeros_like(acc_ref)

    def lane_bcast(ref, h, r, ls):
        word = jnp.broadcast_to(ref[0, h, 0, r:r + 1, ls], (SUBLANES, 128))
        return pltpu.bitcast(word, BF16)

    a_ref[0] = _dot(u_ref[0], h2t_ref[0])
    for bi in range(blocks):
        if bi + 1 < blocks:
            a_ref[bi + 1] = _dot(u_ref[bi + 1], h2t_ref[0])
        for lt in range(tile // 128):
            ls = slice(lt * 128, (lt + 1) * 128)
            gate = [[None] * (N_KEYS // PR) for _ in range(per_block)]
            for h in range(PEER_HEADS):
                cnt = [lane_bcast(c_ref, h, bi * per_block + ii, ls) for ii in range(per_block)]
                p1 = [lane_bcast(p1_ref, h, bi * per_block + ii, ls) for ii in range(per_block)]
                for rb in range(N_KEYS // PR):
                    rs = slice(rb * PR, (rb + 1) * PR)
                    ws = slice(rb * SUBLANES, (rb + 1) * SUBLANES)
                    r2 = pltpu.bitcast(r2_ref[0, h, ws, ls], BF16)
                    q2 = pltpu.bitcast(q2_ref[0, h, ws, ls], BF16)
                    for ii in range(per_block):
                        w = jnp.where(r2 < cnt[ii], q2, jnp.zeros_like(q2)) * p1[ii]
                        gate[ii][rb] = w if gate[ii][rb] is None else gate[ii][rb] + w
            for ii in range(per_block):
                for rb in range(N_KEYS // PR):
                    rs = slice(ii * N_KEYS + rb * PR, ii * N_KEYS + (rb + 1) * PR)
                    ws = slice((ii * N_KEYS + rb * PR) // 2, (ii * N_KEYS + (rb + 1) * PR) // 2)
                    act = _gelu_tanh(a_ref[bi, rs, ls].astype(BF16))
                    hid_ref[bi, ws, ls] = pltpu.bitcast(act * gate[ii][rb], F32)
        acc_ref[...] += _dot(vt_ref[bi], pltpu.bitcast(hid_ref[bi], BF16))

    @pl.when(ci == pl.num_programs(2) - 1)
    def _():
        y = x_ref[0] + mod_ref[0, 5:6, :] * acc_ref[...].T
        if final_norm:
            r = lax.rsqrt(jnp.mean(y * y, axis=-1, keepdims=True) + EPS)
            y = y * r * gf_ref[...]
        o_ref[0] = y


def _peer_dense(h2, r2, q2, cnt, p1, u_blocks, vt_blocks, x, mod, g_final, final_norm, tile=512):
    B, S, D = x.shape
    H = PEER_HEADS
    blocks = CHUNK_KEYS * N_KEYS // EXPERT_BLOCK
    n_chunks = N_KEYS // CHUNK_KEYS
    assert CHUNK_KEYS == SUBLANES
    col = pl.BlockSpec((1, H, N_KEYS // 2, tile), lambda b, i, c: (b, 0, 0, i))
    rowb = pl.BlockSpec((1, H, 1, SUBLANES, tile), lambda b, i, c: (b, 0, c, 0, i))
    tok = pl.BlockSpec((1, tile, D), lambda b, i, c: (b, i, 0))
    return pl.pallas_call(
        functools.partial(_peer_dense_kernel, tile=tile, final_norm=final_norm),
        grid=(B, S // tile, n_chunks),
        in_specs=[
            pl.BlockSpec((1, D, tile), lambda b, i, c: (b, 0, i)), col, col, rowb, rowb,
            pl.BlockSpec((blocks, EXPERT_BLOCK, D), lambda b, i, c: (c, 0, 0)),
            pl.BlockSpec((blocks, D, EXPERT_BLOCK), lambda b, i, c: (c, 0, 0)),
            tok,
            pl.BlockSpec((1, N_MOD, D), lambda b, i, c: (b, 0, 0)),
            pl.BlockSpec((1, D), lambda b, i, c: (0, 0)),
        ],
        out_specs=tok,
        out_shape=jax.ShapeDtypeStruct((B, S, D), F32),
        scratch_shapes=[
            pltpu.VMEM((D, tile), F32),
            pltpu.VMEM((blocks, EXPERT_BLOCK, tile), F32),
            pltpu.VMEM((blocks, EXPERT_BLOCK // 2, tile), F32),
        ],
        compiler_params=_params(("parallel", "parallel", "arbitrary")),
        name="peer_dense",
    )(h2, r2, q2, cnt, p1, u_blocks, vt_blocks, x, mod, g_final.reshape(1, D))


def _trunk(x, mod_all, layers, g_final, tables):
    depth = len(layers)
    for l, p in enumerate(layers):
        mod = mod_all[l]
        zp, zf, zc, zs = _in_proj(x, mod, p["g_mix"], p["w_in"])
        br_fnet = _fourier_mixer(zf, tables)
        br_pool, br_conf, br_sconv = _local_mix(zp, zc, zs, p["pool_maps"], p["pool_scale"], p["dw_w"],
                                                p["dw_b"], p["ln_g"], p["ln_b"], p["sc_w"])
        x = _merge(x, mod, p["g_mix"], (br_pool, br_fnet, br_conf, br_sconv),
                   p["w_gate"], p["w_branch"], p["w_out"])
        h2, r2, q2, cnt, p1 = _peer_prep(x, mod, p["g_ffn"], p["w_q"], p["keys"])
        x = _peer_dense(h2, r2, q2, cnt, p1, p["u"], p["vt"], x, mod, g_final, l == depth - 1)
    return x


def _layer_params(g_norm_mix, w_in, pool_maps, pool_scale, conv_dw_w, conv_dw_b, conv_ln_g, conv_ln_b,
                  sconv_w, w_branch, w_gate, w_out, g_norm_ffn, w_q, sub_keys, expert_u, expert_v):
    depth = w_in.shape[0]
    nblk = N_EXPERTS // EXPERT_BLOCK
    ng = N_KEYS // (2 * SUBLANES)
    k2 = sub_keys[:, :, 1].reshape(depth, PEER_HEADS, ng, SUBLANES, 2, D_HALF)
    k2 = jnp.swapaxes(k2, 3, 4).reshape(depth, PEER_HEADS, N_KEYS, D_HALF)
    sub_keys = jnp.stack([sub_keys[:, :, 0], k2], axis=2)
    layers = []
    for l in range(depth):
        layers.append(dict(
            g_mix=g_norm_mix[l], g_ffn=g_norm_ffn[l],
            w_in=w_in[l].astype(BF16), pool_maps=pool_maps[l].astype(BF16), pool_scale=pool_scale[l],
            dw_w=conv_dw_w[l], dw_b=conv_dw_b[l], ln_g=conv_ln_g[l], ln_b=conv_ln_b[l], sc_w=sconv_w[l],
            w_gate=w_gate[l].astype(BF16), w_branch=w_branch[l].astype(BF16), w_out=w_out[l].astype(BF16),
            w_q=w_q[l].astype(BF16), keys=sub_keys[l].astype(BF16),
            u=expert_u[l].astype(BF16).reshape(nblk, EXPERT_BLOCK, D_MODEL),
            vt=jnp.swapaxes(expert_v[l].astype(BF16).reshape(nblk, EXPERT_BLOCK, D_MODEL), 1, 2),
        ))
    return layers


def kernel(x_prompt, x_sample, c_prompt, c_sample, w_ada, b_ada, g_norm_mix, w_in, pool_maps, pool_scale,
           conv_dw_w, conv_dw_b, conv_ln_g, conv_ln_b, sconv_w, w_branch, w_gate, w_out, g_norm_ffn, w_q,
           sub_keys, expert_u, expert_v, g_final):
    depth = w_in.shape[0]
    nbp = x_prompt.shape[0]
    c_all = jnp.concatenate([c_prompt, c_sample], axis=0)
    mod_all = _modulation(c_all, w_ada, b_ada).reshape(depth, -1, N_MOD, D_MODEL)
    layers = _layer_params(g_norm_mix, w_in, pool_maps, pool_scale, conv_dw_w, conv_dw_b, conv_ln_g,
                           conv_ln_b, sconv_w, w_branch, w_gate, w_out, g_norm_ffn, w_q, sub_keys,
                           expert_u, expert_v)
    outs = []
    for x, mod in ((x_prompt, mod_all[:, :nbp]), (x_sample, mod_all[:, nbp:])):
        outs.append(_trunk(x, mod, layers, g_final, _dft_tables(x.shape[1])))
    return tuple(outs)
```
